```python
import math
import jax, jax.numpy as jnp
from jax import lax
import numpy as np

D_MODEL = 1024
BATCH = 8
SEQ = 2048
DEPTH = 1

CHUNK = 64
Q_BLOCK = 128
POOL_WINDOWS = (2, 4, 8, 16)
N_POOL_GROUPS = len(POOL_WINDOWS)
POOL_WIDTH = D_MODEL // 2
POOL_GROUP = POOL_WIDTH // N_POOL_GROUPS
N_HEADS = 4
HEAD_DIM = D_MODEL // 16
V_HEAD_DIM = 2 * HEAD_DIM
QK_WIDTH = N_HEADS * 2 * HEAD_DIM
ATTN_WIDTH = N_HEADS * V_HEAD_DIM
ROT_DIM = HEAD_DIM // 4
ROPE_THETA = 500000.0
N_BRANCHES = 2
IN_WIDTH = POOL_WIDTH + 2 * QK_WIDTH + ATTN_WIDTH + N_BRANCHES * D_MODEL
N_EXPERTS = 32
TOP_K = 4
D_EXPERT = D_MODEL
SWIGLU_LIMIT = 7.0
SWIGLU_ALPHA = 1.702
NORM_EPS = 1e-6

kernel_name = "hybrid_pool_diffattn_moe_block"


def rmsnorm(x, g, eps=NORM_EPS):
    xf = x.astype(jnp.float32)
    y = xf * lax.rsqrt(jnp.mean(xf * xf, axis=-1, keepdims=True) + eps)
    return (y * g.astype(jnp.float32)).astype(x.dtype)


def rotary_tables(seq):
    inv_freq = ROPE_THETA ** (-jnp.arange(0, ROT_DIM, 2, dtype=jnp.float32) / ROT_DIM)
    ang = jnp.arange(seq, dtype=jnp.float32)[:, None] * inv_freq[None, :]
    return jnp.cos(ang), jnp.sin(ang)


def apply_partial_rotary(t, cos, sin):
    half = ROT_DIM // 2
    c = cos[None, :, None, None, :].astype(t.dtype)
    s = sin[None, :, None, None, :].astype(t.dtype)
    t1 = t[..., :half]
    t2 = t[..., half:ROT_DIM]
    return jnp.concatenate([t1 * c - t2 * s, t2 * c + t1 * s, t[..., ROT_DIM:]], axis=-1)


def multiscale_pool(u, w_grp, scale):
    b, s, _ = u.shape
    uf = u.astype(jnp.float32)
    count = jnp.arange(1, s + 1, dtype=jnp.float32)[None, :, None]
    outs = []
    for gi, w in enumerate(POOL_WINDOWS):
        ug = uf[..., gi * POOL_GROUP:(gi + 1) * POOL_GROUP]
        c = jnp.cumsum(ug, axis=1)
        c_lag = jnp.pad(c, ((0, 0), (w, 0), (0, 0)))[:, :s]
        mean = (c - c_lag) / jnp.minimum(count, float(w))
        outs.append(mean - ug)
    p = jnp.stack(outs, axis=2).astype(u.dtype)
    y = jnp.einsum('bsgc,gcd->bsgd', p, w_grp).reshape(b, s, POOL_WIDTH)
    return y * scale


def diff_attention(q, k, v, lam):
    s = q.shape[1]
    sm_scale = HEAD_DIM ** -0.5
    chunk_id = jnp.arange(s) // CHUNK
    outs = []
    for i in range(s // Q_BLOCK):
        lo, hi = i * Q_BLOCK, (i + 1) * Q_BLOCK
        qb = q[:, lo:hi]
        kb = k[:, :hi]
        vb = v[:, :hi]
        sc = jnp.einsum('bqhmd,bkhmd->bhmqk', qb, kb).astype(jnp.float32) * sm_scale
        mask = chunk_id[lo:hi, None] >= chunk_id[None, :hi]
        sc = jnp.where(mask[None, None, None], sc, -jnp.inf)
        p = jax.nn.softmax(sc, axis=-1)
        a = p[:, :, 0] - lam * p[:, :, 1]
        outs.append(jnp.einsum('bhqk,bkhd->bqhd', a.astype(v.dtype), vb))
    return jnp.concatenate(outs, axis=1)


def moe_ffn(h, router_w, router_b, w_gate, b_gate, w_up, b_up, w_down, b_down):
    b, s, d = h.shape
    t = h.reshape(-1, d)
    logits = (t @ router_w + router_b).astype(jnp.float32)
    top_v, top_i = lax.top_k(logits, TOP_K)
    top_p = jax.nn.softmax(top_v, axis=-1)
    comb = jnp.sum(jax.nn.one_hot(top_i, N_EXPERTS, dtype=jnp.float32) * top_p[..., None], axis=1)
    comb = comb.astype(h.dtype)
    out = jnp.zeros_like(t)
    for e in range(N_EXPERTS):
        g = jnp.minimum(t @ w_gate[e] + b_gate[e], SWIGLU_LIMIT)
        u = jnp.clip(t @ w_up[e] + b_up[e], -SWIGLU_LIMIT, SWIGLU_LIMIT)
        act = (u + 1.0) * (g * jax.nn.sigmoid(SWIGLU_ALPHA * g))
        out = out + comb[:, e:e + 1] * (act @ w_down[e] + b_down[e])
    return out.reshape(b, s, d)


def setup_inputs(seed: int = 0) -> dict:
    key = jax.random.key(seed)
    ks = jax.random.split(key, 26)
    f32 = jnp.float32
    nrm = lambda k, shape, sc: jax.random.normal(k, shape, f32) * sc
    L, D, E, F = DEPTH, D_MODEL, N_EXPERTS, D_EXPERT
    return {
        "x": nrm(ks[0], (BATCH, SEQ, D), 1.0),
        "mix_norm_g": 1.0 + nrm(ks[1], (L, D), 0.05),
        "w_in": nrm(ks[2], (L, D, IN_WIDTH), D ** -0.5),
        "b_branch_gate": nrm(ks[3], (L, N_BRANCHES * D), 0.02),
        "pool_w": nrm(ks[4], (L, N_POOL_GROUPS, POOL_GROUP, POOL_GROUP), POOL_GROUP ** -0.5),
        "pool_scale": 1.0 + nrm(ks[5], (L, POOL_WIDTH), 0.1),
        "pool_up": nrm(ks[6], (L, POOL_WIDTH, D), POOL_WIDTH ** -0.5),
        "q_norm_g": 1.0 + nrm(ks[7], (L, HEAD_DIM), 0.05),
        "k_norm_g": 1.0 + nrm(ks[8], (L, HEAD_DIM), 0.05),
        "lambda_q1": nrm(ks[9], (L, HEAD_DIM), 0.1),
        "lambda_k1": nrm(ks[10], (L, HEAD_DIM), 0.1),
        "lambda_q2": nrm(ks[11], (L, HEAD_DIM), 0.1),
        "lambda_k2": nrm(ks[12], (L, HEAD_DIM), 0.1),
        "subln_g": 1.0 + nrm(ks[13], (L, V_HEAD_DIM), 0.05),
        "attn_up": nrm(ks[14], (L, ATTN_WIDTH, D), ATTN_WIDTH ** -0.5),
        "w_out": nrm(ks[15], (L, D, D), D ** -0.5),
        "ffn_norm_g": 1.0 + nrm(ks[16], (L, D), 0.05),
        "router_w": nrm(ks[17], (L, D, E), D ** -0.5),
        "router_b": nrm(ks[18], (L, E), 0.01),
        "exp_w_gate": nrm(ks[19], (L, E, D, F), D ** -0.5),
        "exp_b_gate": nrm(ks[20], (L, E, F), 0.02),
        "exp_w_up": nrm(ks[21], (L, E, D, F), D ** -0.5),
        "exp_b_up": nrm(ks[22], (L, E, F), 0.02),
        "exp_w_down": nrm(ks[23], (L, E, F, D), F ** -0.5),
        "exp_b_down": nrm(ks[24], (L, E, D), 0.02),
    }


def reference(x, mix_norm_g, w_in, b_branch_gate, pool_w, pool_scale, pool_up, q_norm_g, k_norm_g,
              lambda_q1, lambda_k1, lambda_q2, lambda_k2, subln_g, attn_up, w_out, ffn_norm_g,
              router_w, router_b, exp_w_gate, exp_b_gate, exp_w_up, exp_b_up, exp_w_down, exp_b_down):
    b, s, d = x.shape
    cos, sin = rotary_tables(s)
    o_q = POOL_WIDTH
    o_k = o_q + QK_WIDTH
    o_v = o_k + QK_WIDTH
    o_g = o_v + ATTN_WIDTH
    for l in range(DEPTH):
        h = rmsnorm(x, mix_norm_g[l])
        proj = h @ w_in[l]
        u_pool = proj[..., :o_q]
        q = proj[..., o_q:o_k].reshape(b, s, N_HEADS, 2, HEAD_DIM)
        k = proj[..., o_k:o_v].reshape(b, s, N_HEADS, 2, HEAD_DIM)
        v = proj[..., o_v:o_g].reshape(b, s, N_HEADS, V_HEAD_DIM)
        gates = jax.nn.sigmoid(proj[..., o_g:] + b_branch_gate[l])
        g_pool, g_attn = gates[..., :d], gates[..., d:]

        y_pool = multiscale_pool(u_pool, pool_w[l], pool_scale[l]) @ pool_up[l]

        q = apply_partial_rotary(rmsnorm(q, q_norm_g[l]), cos, sin)
        k = apply_partial_rotary(rmsnorm(k, k_norm_g[l]), cos, sin)
        lam_init = 0.8 - 0.6 * math.exp(-0.3 * l)
        lam = (jnp.exp(jnp.sum(lambda_q1[l].astype(jnp.float32) * lambda_k1[l].astype(jnp.float32)))
               - jnp.exp(jnp.sum(lambda_q2[l].astype(jnp.float32) * lambda_k2[l].astype(jnp.float32)))
               + lam_init)
        o = diff_attention(q, k, v, lam)
        o = rmsnorm(o, subln_g[l]) * (1.0 - lam_init)
        y_attn = o.reshape(b, s, ATTN_WIDTH) @ attn_up[l]

        merged = g_pool * y_pool + g_attn * y_attn
        x = x + merged @ w_out[l]

        h2 = rmsnorm(x, ffn_norm_g[l])
        x = x + moe_ffn(h2, router_w[l], router_b[l], exp_w_gate[l], exp_b_gate[l],
                        exp_w_up[l], exp_b_up[l], exp_w_down[l], exp_b_down[l])
    return x
```

```python
import functools
import math

import jax
import jax.numpy as jnp
from jax import lax
from jax.experimental import pallas as pl
from jax.experimental.pallas import tpu as pltpu

F32 = jnp.float32
BF16 = jnp.bfloat16
I32 = jnp.int32

CHUNK = 64
POOL_WINDOWS = (2, 4, 8, 16)
MAX_WINDOW = max(POOL_WINDOWS)
N_HEADS = 4
HEAD_DIM = 64
ROT_DIM = HEAD_DIM // 4
ROPE_THETA = 500000.0
N_EXPERTS = 32
TOP_K = 4
SWIGLU_LIMIT = 7.0
SWIGLU_ALPHA = 1.702
NORM_EPS = 1e-6
MASK_VALUE = -1e30

LANES = 128
VMEM_LIMIT_BYTES = 56 * 1024 * 1024
TM_IN = 512
TQ = 256
TM_EXP = 256


def _cparams(sem):
    return pltpu.CompilerParams(dimension_semantics=sem, vmem_limit_bytes=VMEM_LIMIT_BYTES)


def _dot(a, b):
    return jnp.dot(a, b, preferred_element_type=F32)


def _mixer_in_kernel(x_ref, g_ref, w_ref, bg_ref, pw_ref, ps_ref, pu_ref, qg_ref, kg_ref,
                     cos_ref, sa_ref, sb_ref, seg_ref,
                     q_ref, kt_ref, v_ref, gp_ref, ga_ref, carry_ref):
    si = pl.program_id(1)
    tm, d = x_ref.shape
    pw_width = ps_ref.shape[1]
    qk_width = q_ref.shape[1]
    tk = kt_ref.shape[-1]

    x = x_ref[...]
    ms = jnp.mean(x * x, axis=-1, keepdims=True)
    h = (x * lax.rsqrt(ms + NORM_EPS) * g_ref[...]).astype(BF16)

    def proj(lo, hi):
        return _dot(h, w_ref[:, lo:hi])

    o_q = pw_width
    o_k = o_q + qk_width
    o_v = o_k + qk_width
    o_g = o_v + v_ref.shape[1]

    u = proj(0, o_q)

    @pl.when(si == 0)
    def _():
        carry_ref[...] = jnp.zeros_like(carry_ref)

    ext = jnp.concatenate([carry_ref[...], u], axis=0)
    carry_ref[...] = u[tm - MAX_WINDOW:, :]
    pos = si * tm + lax.broadcasted_iota(I32, (tm, 1), 0)
    ys = []
    for gi, win in enumerate(POOL_WINDOWS):
        s = ext[:, gi * LANES:(gi + 1) * LANES]
        shift = 1
        while shift < win:
            s = s + pltpu.roll(s, shift, axis=0)
            shift *= 2
        cnt = jnp.minimum(pos + 1, win).astype(F32)
        p = s[MAX_WINDOW:, :] / cnt - u[:, gi * LANES:(gi + 1) * LANES]
        ys.append(_dot(p.astype(BF16), pw_ref[gi]))
    yp = jnp.concatenate(ys, axis=1) * ps_ref[...]
    y_pool = _dot(yp.astype(BF16), pu_ref[...])

    g_pool = jax.nn.sigmoid(proj(o_g, o_g + d) + bg_ref[:, :d])
    gp_ref[...] = (g_pool * y_pool).astype(BF16)
    ga_ref[...] = jax.nn.sigmoid(proj(o_g + d, o_g + 2 * d) + bg_ref[:, d:]).astype(BF16)

    def qk_prep(t, gain_ref, scale):
        outs = []
        for c in range(qk_width // LANES):
            tc = t[:, c * LANES:(c + 1) * LANES]
            seg_ms = _dot((tc * tc).astype(BF16), seg_ref[...])
            n = tc * lax.rsqrt(seg_ms + NORM_EPS) * gain_ref[...]
            r = (n * cos_ref[...]
                 + pltpu.roll(n, LANES - ROT_DIM // 2, axis=1) * sa_ref[...]
                 + pltpu.roll(n, ROT_DIM // 2, axis=1) * sb_ref[...])
            outs.append(r * scale)
        return jnp.concatenate(outs, axis=1)

    q = qk_prep(proj(o_q, o_k), qg_ref, HEAD_DIM ** -0.5)
    q_ref[...] = q.astype(BF16)
    k = qk_prep(proj(o_k, o_v), kg_ref, 1.0)
    kt = k.T
    for j in range(tm // tk):
        kt_ref[j] = kt[:, j * tk:(j + 1) * tk].astype(BF16)
    v_ref[...] = proj(o_v, o_g).astype(BF16)


def _mixer_in(xf, g, w_in, b_gate, pool_w, pool_scale, pool_up, qg, kg, cos_t, sa_t, sb_t, seg,
              batch, seq):
    n, d = xf.shape
    pw = pool_scale.shape[1]
    qkw = (w_in.shape[1] - pw - 2 * d) // 3
    nst = seq // TM_IN
    row = lambda b, s: (b * nst + s, 0)
    const2 = lambda b, s: (0, 0)
    return pl.pallas_call(
        _mixer_in_kernel,
        grid=(batch, nst),
        in_specs=[
            pl.BlockSpec((TM_IN, d), row),
            pl.BlockSpec((1, d), const2),
            pl.BlockSpec(w_in.shape, const2),
            pl.BlockSpec(b_gate.shape, const2),
            pl.BlockSpec(pool_w.shape, lambda b, s: (0, 0, 0)),
            pl.BlockSpec(pool_scale.shape, const2),
            pl.BlockSpec(pool_up.shape, const2),
            pl.BlockSpec((1, LANES), const2),
            pl.BlockSpec((1, LANES), const2),
            pl.BlockSpec((TM_IN, LANES), lambda b, s: (s, 0)),
            pl.BlockSpec((TM_IN, LANES), lambda b, s: (s, 0)),
            pl.BlockSpec((TM_IN, LANES), lambda b, s: (s, 0)),
            pl.BlockSpec((LANES, LANES), const2),
        ],
        out_specs=[
            pl.BlockSpec((TM_IN, qkw), row),
            pl.BlockSpec((None, TM_IN // TQ, qkw, TQ), lambda b, s: (b, s, 0, 0)),
            pl.BlockSpec((TM_IN, qkw), row),
            pl.BlockSpec((TM_IN, d), row),
            pl.BlockSpec((TM_IN, d), row),
        ],
        out_shape=[
            jax.ShapeDtypeStruct((n, qkw), BF16),
            jax.ShapeDtypeStruct((batch, seq // TQ, qkw, TQ), BF16),
            jax.ShapeDtypeStruct((n, qkw), BF16),
            jax.ShapeDtypeStruct((n, d), BF16),
            jax.ShapeDtypeStruct((n, d), BF16),
        ],
        scratch_shapes=[pltpu.VMEM((MAX_WINDOW, pw), F32)],
        compiler_params=_cparams(("arbitrary", "arbitrary")),
        name="mixer_in",
    )(xf, g, w_in, b_gate, pool_w, pool_scale, pool_up, qg, kg, cos_t, sa_t, sb_t, seg)


def _attn_route_kernel(lam_init, lam_ref, q_ref, kt_ref, v_ref, x_ref, gp_ref, ga_ref, sg_ref,
                       au_ref, wo_ref, fg_ref, rwh_ref, rwl_ref, rb_ref,
                       x1_ref, h2_ref, ti_ref, tp_ref, rk_ref, cnt_ref,
                       acc_ref, o_ref, carry_ref):
    b = pl.program_id(0)
    qi = pl.program_id(1)
    tq = q_ref.shape[0]
    tk = kt_ref.shape[-1]
    n_exp = rb_ref.shape[1]

    lam_p = lam_ref[...]
    lam = (jnp.exp(jnp.sum(lam_p[0:1] * lam_p[1:2], axis=1, keepdims=True))
           - jnp.exp(jnp.sum(lam_p[2:3] * lam_p[3:4], axis=1, keepdims=True))
           + lam_init)

    lane = lax.broadcasted_iota(I32, (tq, LANES), 1)
    r_chunk = lax.broadcasted_iota(I32, (tq, tk), 0) // CHUNK
    c_chunk = lax.broadcasted_iota(I32, (tq, tk), 1) // CHUNK
    diag_mask = r_chunk >= c_chunk

    for hd in range(N_HEADS):
        cols = slice(hd * LANES, (hd + 1) * LANES)
        qh = q_ref[:, cols]
        zero = jnp.zeros_like(qh)
        qms = (jnp.where(lane < HEAD_DIM, qh, zero), jnp.where(lane >= HEAD_DIM, qh, zero))

        ktd = kt_ref[qi, cols, :]
        vd = v_ref[pl.ds(pl.multiple_of(qi * tk, tk), tk), cols]
        state = []
        for m in range(2):
            s = jnp.where(diag_mask, _dot(qms[m], ktd), MASK_VALUE)
            mx = jnp.max(s, axis=1, keepdims=True)
            p = jnp.exp(s - mx)
            state += [mx, jnp.sum(p, axis=1, keepdims=True)]
            acc_ref[m] = _dot(p.astype(BF16), vd)

        def body(j, carry, qms=qms, cols=cols):
            ktj = kt_ref[j, cols, :]
            vj = v_ref[pl.ds(pl.multiple_of(j * tk, tk), tk), cols]
            new = []
            for m in range(2):
                m_old, l_old = carry[2 * m], carry[2 * m + 1]
                s = _dot(qms[m], ktj)
                m_new = jnp.maximum(m_old, jnp.max(s, axis=1, keepdims=True))
                alpha = jnp.exp(m_old - m_new)
                p = jnp.exp(s - m_new)
                new += [m_new, alpha * l_old + jnp.sum(p, axis=1, keepdims=True)]
                acc_ref[m] = alpha * acc_ref[m] + _dot(p.astype(BF16), vj)
            return tuple(new)

        _, l1, _, l2 = lax.fori_loop(0, qi, body, tuple(state))
        o = acc_ref[0] / l1 - lam * (acc_ref[1] / l2)
        o_ms = jnp.mean(o * o, axis=1, keepdims=True)
        o_ref[:, cols] = o * lax.rsqrt(o_ms + NORM_EPS) * sg_ref[...] * (1.0 - lam_init)

    y_attn = _dot(o_ref[...].astype(BF16), au_ref[...])
    merged = gp_ref[...].astype(F32) + ga_ref[...].astype(F32) * y_attn
    x1 = x_ref[...] + _dot(merged.astype(BF16), wo_ref[...])
    x1_ref[...] = x1

    x_ms = jnp.mean(x1 * x1, axis=1, keepdims=True)
    h2 = x1 * lax.rsqrt(x_ms + NORM_EPS) * fg_ref[...]
    h2_ref[...] = h2
    hh = h2.astype(BF16)
    hl = (h2 - hh.astype(F32)).astype(BF16)
    logits = (_dot(hh, rwh_ref[...]) + _dot(hh, rwl_ref[...]) + _dot(hl, rwh_ref[...])
              + rb_ref[...])

    e_iota = lax.broadcasted_iota(I32, (tq, n_exp), 1)
    work = logits
    vals, idxs, hots = [], [], []
    for _ in range(TOP_K):
        mx = jnp.max(work, axis=1, keepdims=True)
        idx = jnp.min(jnp.where(work == mx, e_iota, n_exp), axis=1, keepdims=True)
        hot = e_iota == idx
        work = jnp.where(hot, -jnp.inf, work)
        vals.append(mx)
        idxs.append(idx)
        hots.append(hot)
    exps = [jnp.exp(v - vals[0]) for v in vals]
    den = exps[0] + exps[1] + exps[2] + exps[3]

    @pl.when(jnp.logical_and(b == 0, qi == 0))
    def _():
        carry_ref[...] = jnp.zeros_like(carry_ref)

    sel = (hots[0].astype(F32) + hots[1].astype(F32) + hots[2].astype(F32) + hots[3].astype(F32))
    tri = (lax.broadcasted_iota(I32, (tq, tq), 0) > lax.broadcasted_iota(I32, (tq, tq), 1))
    before = _dot(tri.astype(BF16), sel.astype(BF16)) + carry_ref[...]
    carry_ref[...] = carry_ref[...] + jnp.sum(sel, axis=0, keepdims=True)
    cnt_ref[...] = carry_ref[...]

    k_iota = lax.broadcasted_iota(I32, (tq, TOP_K), 1)
    ti = jnp.zeros((tq, TOP_K), I32)
    tp = jnp.zeros((tq, TOP_K), F32)
    rk = jnp.zeros((tq, TOP_K), I32)
    for k in range(TOP_K):
        rank_k = jnp.sum(jnp.where(hots[k], before, 0.0), axis=1, keepdims=True).astype(I32)
        ti = jnp.where(k_iota == k, idxs[k], ti)
        tp = jnp.where(k_iota == k, exps[k] / den, tp)
        rk = jnp.where(k_iota == k, rank_k, rk)
    ti_ref[...] = ti
    tp_ref[...] = tp
    rk_ref[...] = rk


def _attn_route(lam_init, lam_p, q, kt, v, xf, gp, ga, sg, attn_up, w_out, fg, rwh, rwl, rb,
                batch, seq):
    n, d = xf.shape
    qkw = q.shape[1]
    nqt = seq // TQ
    n_exp = rb.shape[1]
    row = lambda b, i: (b * nqt + i, 0)
    const2 = lambda b, i: (0, 0)
    return pl.pallas_call(
        functools.partial(_attn_route_kernel, lam_init),
        grid=(batch, nqt),
        in_specs=[
            pl.BlockSpec(lam_p.shape, const2),
            pl.BlockSpec((TQ, qkw), row),
            pl.BlockSpec((None, nqt, qkw, TQ), lambda b, i: (b, 0, 0, 0)),
            pl.BlockSpec((seq, qkw), lambda b, i: (b, 0)),
            pl.BlockSpec((TQ, d), row),
            pl.BlockSpec((TQ, d), row),
            pl.BlockSpec((TQ, d), row),
            pl.BlockSpec((1, LANES), const2),
            pl.BlockSpec(attn_up.shape, const2),
            pl.BlockSpec(w_out.shape, const2),
            pl.BlockSpec((1, d), const2),
            pl.BlockSpec(rwh.shape, const2),
            pl.BlockSpec(rwl.shape, const2),
            pl.BlockSpec(rb.shape, const2),
        ],
        out_specs=[
            pl.BlockSpec((TQ, d), row),
            pl.BlockSpec((TQ, d), row),
            pl.BlockSpec((TQ, TOP_K), row),
            pl.BlockSpec((TQ, TOP_K), row),
            pl.BlockSpec((TQ, TOP_K), row),
            pl.BlockSpec((1, n_exp), const2),
        ],
        out_shape=[
            jax.ShapeDtypeStruct((n, d), F32),
            jax.ShapeDtypeStruct((n, d), F32),
            jax.ShapeDtypeStruct((n, TOP_K), I32),
            jax.ShapeDtypeStruct((n, TOP_K), F32),
            jax.ShapeDtypeStruct((n, TOP_K), I32),
            jax.ShapeDtypeStruct((1, n_exp), F32),
        ],
        scratch_shapes=[
            pltpu.VMEM((2, TQ, LANES), F32),
            pltpu.VMEM((TQ, qkw), F32),
            pltpu.VMEM((1, n_exp), F32),
        ],
        compiler_params=_cparams(("arbitrary", "arbitrary")),
        name="attn_route",
    )(lam_p, q, kt, v, xf, gp, ga, sg, attn_up, w_out, fg, rwh, rwl, rb)


def _row_copy(src, dst, sem):
    return pltpu.make_async_copy(src, dst, sem)


def _push_rows_kernel(dest_ref, h_ref, xs_in_ref, xs_ref, sem):
    del xs_in_ref
    tq = h_ref.shape[0]
    base = pl.program_id(0) * (tq * TOP_K)

    def issue(r, c):
        for k in range(TOP_K):
            d = dest_ref[base + r * TOP_K + k]
            _row_copy(h_ref.at[pl.ds(r, 1)], xs_ref.at[pl.ds(d, 1)], sem).start()
        return c

    lax.fori_loop(0, tq, issue, 0)

    def drain(r, c):
        for k in range(TOP_K):
            _row_copy(h_ref.at[pl.ds(0, 1)], xs_ref.at[pl.ds(0, 1)], sem).wait()
        return c

    lax.fori_loop(0, tq, drain, 0)


def _push_rows(dest, h2, xs_init):
    n, d = h2.shape
    return pl.pallas_call(
        _push_rows_kernel,
        grid_spec=pltpu.PrefetchScalarGridSpec(
            num_scalar_prefetch=1,
            grid=(n // TQ,),
            in_specs=[
                pl.BlockSpec((TQ, d), lambda i, dest: (i, 0)),
                pl.BlockSpec(memory_space=pl.ANY),
            ],
            out_specs=pl.BlockSpec(memory_space=pl.ANY),
            scratch_shapes=[pltpu.SemaphoreType.DMA(())],
        ),
        out_shape=jax.ShapeDtypeStruct(xs_init.shape, xs_init.dtype),
        input_output_aliases={2: 0},
        compiler_params=_cparams(("arbitrary",)),
        name="push_rows",
    )(dest, h2, xs_init)


def _expert_ffn_kernel(te_ref, nv_ref, x_ref, wg_ref, bg_ref, wu_ref, bu_ref, wd_ref, bd_ref,
                       y_ref, wgb_ref, wub_ref, wdb_ref):
    i = pl.program_id(0)

    @pl.when(i >= nv_ref[0])
    def _():
        y_ref[...] = jnp.zeros_like(y_ref)

    @pl.when(i < nv_ref[0])
    def _():
        prev = te_ref[jnp.maximum(i - 1, 0)]

        @pl.when(jnp.logical_or(i == 0, te_ref[i] != prev))
        def _():
            wgb_ref[...] = wg_ref[...].astype(BF16)
            wub_ref[...] = wu_ref[...].astype(BF16)
            wdb_ref[...] = wd_ref[...].astype(BF16)

        x = x_ref[...].astype(BF16)
        g = jnp.minimum(_dot(x, wgb_ref[...]) + bg_ref[...], SWIGLU_LIMIT)
        u = jnp.clip(_dot(x, wub_ref[...]) + bu_ref[...], -SWIGLU_LIMIT, SWIGLU_LIMIT)
        act = (u + 1.0) * (g * jax.nn.sigmoid(SWIGLU_ALPHA * g))
        y_ref[...] = _dot(act.astype(BF16), wdb_ref[...]) + bd_ref[...]


def _expert_ffn(tile_expert, n_valid, xs, w_gate, b_gate, w_up, b_up, w_down, b_down):
    p_rows, d = xs.shape
    n_exp, _, f = w_gate.shape
    wspec = lambda shape: pl.BlockSpec((None,) + shape, lambda i, te, nv: (te[i], 0, 0))
    return pl.pallas_call(
        _expert_ffn_kernel,
        grid_spec=pltpu.PrefetchScalarGridSpec(
            num_scalar_prefetch=2,
            grid=(p_rows // TM_EXP,),
            in_specs=[
                pl.BlockSpec((TM_EXP, d), lambda i, te, nv: (i, 0)),
                wspec((d, f)), wspec((1, f)),
                wspec((d, f)), wspec((1, f)),
                wspec((f, d)), wspec((1, d)),
            ],
            out_specs=pl.BlockSpec((TM_EXP, d), lambda i, te, nv: (i, 0)),
            scratch_shapes=[
                pltpu.VMEM((d, f), BF16),
                pltpu.VMEM((d, f), BF16),
                pltpu.VMEM((f, d), BF16),
            ],
        ),
        out_shape=jax.ShapeDtypeStruct((p_rows, d), F32),
        compiler_params=_cparams(("arbitrary",)),
        name="expert_ffn",
    )(tile_expert, n_valid, xs, w_gate, b_gate.reshape(n_exp, 1, f), w_up,
      b_up.reshape(n_exp, 1, f), w_down, b_down.reshape(n_exp, 1, d))


def _combine_kernel(dest_ref, x1_ref, tp_ref, y_ref, out_ref, buf_ref, sem):
    tq = x1_ref.shape[0]
    base = pl.program_id(0) * (tq * TOP_K)

    def issue(r, c):
        for k in range(TOP_K):
            d = dest_ref[base + r * TOP_K + k]
            _row_copy(y_ref.at[pl.ds(d, 1)], buf_ref.at[k, pl.ds(r, 1)], sem).start()
        return c

    lax.fori_loop(0, tq, issue, 0)

    def drain(r, c):
        for k in range(TOP_K):
            _row_copy(y_ref.at[pl.ds(0, 1)], buf_ref.at[0, pl.ds(0, 1)], sem).wait()
        return c

    lax.fori_loop(0, tq, drain, 0)

    tp = tp_ref[...]
    out = x1_ref[...]
    for k in range(TOP_K):
        out = out + tp[:, k:k + 1] * buf_ref[k]
    out_ref[...] = out


def _combine(dest, x1, tp, y):
    n, d = x1.shape
    return pl.pallas_call(
        _combine_kernel,
        grid_spec=pltpu.PrefetchScalarGridSpec(
            num_scalar_prefetch=1,
            grid=(n // TQ,),
            in_specs=[
                pl.BlockSpec((TQ, d), lambda i, dest: (i, 0)),
                pl.BlockSpec((TQ, TOP_K), lambda i, dest: (i, 0)),
                pl.BlockSpec(memory_space=pl.ANY),
            ],
            out_specs=pl.BlockSpec((TQ, d), lambda i, dest: (i, 0)),
            scratch_shapes=[pltpu.VMEM((TOP_K, TQ, d), F32), pltpu.SemaphoreType.DMA(())],
        ),
        out_shape=jax.ShapeDtypeStruct((n, d), F32),
        compiler_params=_cparams(("arbitrary",)),
        name="combine",
    )(dest, x1, tp, y)


def _rotary_lane_tables(seq):
    half = ROT_DIM // 2
    inv_freq = ROPE_THETA ** (-jnp.arange(0, ROT_DIM, 2, dtype=F32) / ROT_DIM)
    ang = jnp.arange(seq, dtype=F32)[:, None] * inv_freq[None, :]
    cos, sin = jnp.cos(ang), jnp.sin(ang)
    seg_pos = jnp.arange(LANES) % HEAD_DIM
    fidx = seg_pos % half
    cos_l, sin_l = cos[:, fidx], sin[:, fidx]
    lo = (seg_pos < half)[None, :]
    hi = jnp.logical_and(seg_pos >= half, seg_pos < ROT_DIM)[None, :]
    cos_t = jnp.where(jnp.logical_or(lo, hi), cos_l, 1.0)
    sa_t = jnp.where(lo, -sin_l, 0.0)
    sb_t = jnp.where(hi, sin_l, 0.0)
    return cos_t.astype(F32), sa_t.astype(F32), sb_t.astype(F32)


def _routing_tables(ti, rk, cnt, n_tiles):
    counts = cnt[0].astype(I32)
    tiles_e = (counts + TM_EXP - 1) // TM_EXP
    tile_end = jnp.cumsum(tiles_e)
    group_start = (tile_end - tiles_e) * TM_EXP
    e_ids = jnp.arange(N_EXPERTS, dtype=I32)
    start_of = jnp.sum(jnp.where(ti[..., None] == e_ids, group_start, 0), axis=-1)
    dest = (start_of + rk).reshape(-1).astype(I32)
    n_valid = tile_end[-1:].astype(I32)
    tids = jnp.arange(n_tiles, dtype=I32)
    tile_expert = jnp.sum((jnp.minimum(tids, n_valid[0] - 1)[:, None] >= tile_end[None, :]).astype(I32),
                          axis=1).astype(I32)
    return dest, tile_expert, n_valid


def kernel(x, mix_norm_g, w_in, b_branch_gate, pool_w, pool_scale, pool_up, q_norm_g, k_norm_g,
           lambda_q1, lambda_k1, lambda_q2, lambda_k2, subln_g, attn_up, w_out, ffn_norm_g,
           router_w, router_b, exp_w_gate, exp_b_gate, exp_w_up, exp_b_up, exp_w_down, exp_b_down):
    batch, seq, d = x.shape
    n = batch * seq
    depth = w_in.shape[0]
    assert seq % TM_IN == 0 and TM_IN % TQ == 0 and HEAD_DIM * 2 == LANES
    assert (n * TOP_K) % TM_EXP == 0
    n_tiles = (n * TOP_K) // TM_EXP + N_EXPERTS

    cos_t, sa_t, sb_t = _rotary_lane_tables(seq)
    seg = jnp.where((jnp.arange(LANES)[:, None] // HEAD_DIM) == (jnp.arange(LANES)[None, :] // HEAD_DIM),
                    1.0 / HEAD_DIM, 0.0).astype(BF16)
    tile_lanes = lambda g: jnp.tile(g, LANES // HEAD_DIM)[None, :]

    xf = x.reshape(n, d)
    for l in range(depth):
        lam_init = 0.8 - 0.6 * math.exp(-0.3 * l)
        q, kt, v, gp, ga = _mixer_in(
            xf, mix_norm_g[l][None, :], w_in[l].astype(BF16), b_branch_gate[l][None, :],
            pool_w[l].astype(BF16), pool_scale[l][None, :], pool_up[l].astype(BF16),
            tile_lanes(q_norm_g[l]), tile_lanes(k_norm_g[l]), cos_t, sa_t, sb_t, seg, batch, seq)

        lam_p = jnp.stack([lambda_q1[l], lambda_k1[l], lambda_q2[l], lambda_k2[l]]).astype(F32)
        rw = router_w[l]
        rwh = rw.astype(BF16)
        rwl = (rw - rwh.astype(F32)).astype(BF16)
        x1, h2, ti, tp, rk, cnt = _attn_route(
            lam_init, lam_p, q, kt, v, xf, gp, ga, subln_g[l][None, :], attn_up[l].astype(BF16),
            w_out[l].astype(BF16), ffn_norm_g[l][None, :], rwh, rwl, router_b[l][None, :],
            batch, seq)

        dest, tile_expert, n_valid = _routing_tables(ti, rk, cnt, n_tiles)
        xs = _push_rows(dest, h2, jnp.zeros((n_tiles * TM_EXP, d), F32))
        y = _expert_ffn(tile_expert, n_valid, xs, exp_w_gate[l], exp_b_gate[l], exp_w_up[l],
                        exp_b_up[l], exp_w_down[l], exp_b_down[l])
        xf = _combine(dest, x1, tp, y)
    return xf.reshape(batch, seq, d)
```

```python
import functools
import math

import jax
import jax.numpy as jnp
from jax import lax
from jax.experimental import pallas as pl
from jax.experimental.pallas import tpu as pltpu

F32 = jnp.float32
BF16 = jnp.bfloat16
I32 = jnp.int32

CHUNK = 64
POOL_WINDOWS = (2, 4, 8, 16)
MAX_WINDOW = max(POOL_WINDOWS)
N_HEADS = 4
HEAD_DIM = 64
ROT_DIM = HEAD_DIM // 4
ROPE_THETA = 500000.0
N_EXPERTS = 32
TOP_K = 4
SWIGLU_LIMIT = 7.0
SWIGLU_ALPHA = 1.702
NORM_EPS = 1e-6
MASK_VALUE = -1e30

LANES = 128
SUBLANES = 8
VMEM_LIMIT_BYTES = 56 * 1024 * 1024
TM_IN = 512
TQ = 256
TM_EXP = 256


def _cparams(sem):
    return pltpu.CompilerParams(dimension_semantics=sem, vmem_limit_bytes=VMEM_LIMIT_BYTES)


def _dot(a, b):
    return jnp.dot(a, b, preferred_element_type=F32)


def _store_row_tiles(ref, val):
    rows, d = val.shape
    assert d == SUBLANES * LANES and ref.shape == (rows * SUBLANES, LANES)
    for j in range(SUBLANES):
        ref[pl.ds(j, rows, stride=SUBLANES), :] = val[:, j * LANES:(j + 1) * LANES]


def _load_row_tile_piece(ref, j):
    return ref[pl.ds(j, ref.shape[0] // SUBLANES, stride=SUBLANES), :]


def _mixer_in_kernel(x_ref, g_ref, w_ref, bg_ref, pw_ref, ps_ref, pu_ref, qg_ref, kg_ref,
                     cos_ref, sa_ref, sb_ref, seg_ref,
                     q_ref, kt_ref, v_ref, gp_ref, ga_ref, carry_ref):
    si = pl.program_id(1)
    tm, d = x_ref.shape
    pw_width = ps_ref.shape[1]
    qk_width = q_ref.shape[1]
    tk = kt_ref.shape[-1]

    x = x_ref[...]
    ms = jnp.mean(x * x, axis=-1, keepdims=True)
    h = (x * lax.rsqrt(ms + NORM_EPS) * g_ref[...]).astype(BF16)

    def proj(lo, hi):
        return _dot(h, w_ref[:, lo:hi])

    o_q = pw_width
    o_k = o_q + qk_width
    o_v = o_k + qk_width
    o_g = o_v + v_ref.shape[1]

    u = proj(0, o_q)

    @pl.when(si == 0)
    def _():
        carry_ref[...] = jnp.zeros_like(carry_ref)

    ext = jnp.concatenate([carry_ref[...], u], axis=0)
    carry_ref[...] = u[tm - MAX_WINDOW:, :]
    pos = si * tm + lax.broadcasted_iota(I32, (tm, 1), 0)
    ys = []
    for gi, win in enumerate(POOL_WINDOWS):
        s = ext[:, gi * LANES:(gi + 1) * LANES]
        shift = 1
        while shift < win:
            s = s + pltpu.roll(s, shift, axis=0)
            shift *= 2
        cnt = jnp.minimum(pos + 1, win).astype(F32)
        p = s[MAX_WINDOW:, :] / cnt - u[:, gi * LANES:(gi + 1) * LANES]
        ys.append(_dot(p.astype(BF16), pw_ref[gi]))
    yp = jnp.concatenate(ys, axis=1) * ps_ref[...]
    y_pool = _dot(yp.astype(BF16), pu_ref[...])

    g_pool = jax.nn.sigmoid(proj(o_g, o_g + d) + bg_ref[:, :d])
    gp_ref[...] = (g_pool * y_pool).astype(BF16)
    ga_ref[...] = jax.nn.sigmoid(proj(o_g + d, o_g + 2 * d) + bg_ref[:, d:]).astype(BF16)

    def qk_prep(t, gain_ref, scale):
        outs = []
        for c in range(qk_width // LANES):
            tc = t[:, c * LANES:(c + 1) * LANES]
            seg_ms = _dot((tc * tc).astype(BF16), seg_ref[...])
            n = tc * lax.rsqrt(seg_ms + NORM_EPS) * gain_ref[...]
            r = (n * cos_ref[...]
                 + pltpu.roll(n, LANES - ROT_DIM // 2, axis=1) * sa_ref[...]
                 + pltpu.roll(n, ROT_DIM // 2, axis=1) * sb_ref[...])
            outs.append(r * scale)
        return jnp.concatenate(outs, axis=1)

    q = qk_prep(proj(o_q, o_k), qg_ref, HEAD_DIM ** -0.5)
    q_ref[...] = q.astype(BF16)
    k = qk_prep(proj(o_k, o_v), kg_ref, 1.0)
    kt = k.T
    for j in range(tm // tk):
        kt_ref[j] = kt[:, j * tk:(j + 1) * tk].astype(BF16)
    v_ref[...] = proj(o_v, o_g).astype(BF16)


def _mixer_in(xf, g, w_in, b_gate, pool_w, pool_scale, pool_up, qg, kg, cos_t, sa_t, sb_t, seg,
              batch, seq):
    n, d = xf.shape
    pw = pool_scale.shape[1]
    qkw = (w_in.shape[1] - pw - 2 * d) // 3
    nst = seq // TM_IN
    row = lambda b, s: (b * nst + s, 0)
    const2 = lambda b, s: (0, 0)
    return pl.pallas_call(
        _mixer_in_kernel,
        grid=(batch, nst),
        in_specs=[
            pl.BlockSpec((TM_IN, d), row),
            pl.BlockSpec((1, d), const2),
            pl.BlockSpec(w_in.shape, const2),
            pl.BlockSpec(b_gate.shape, const2),
            pl.BlockSpec(pool_w.shape, lambda b, s: (0, 0, 0)),
            pl.BlockSpec(pool_scale.shape, const2),
            pl.BlockSpec(pool_up.shape, const2),
            pl.BlockSpec((1, LANES), const2),
            pl.BlockSpec((1, LANES), const2),
            pl.BlockSpec((TM_IN, LANES), lambda b, s: (s, 0)),
            pl.BlockSpec((TM_IN, LANES), lambda b, s: (s, 0)),
            pl.BlockSpec((TM_IN, LANES), lambda b, s: (s, 0)),
            pl.BlockSpec((LANES, LANES), const2),
        ],
        out_specs=[
            pl.BlockSpec((TM_IN, qkw), row),
            pl.BlockSpec((None, TM_IN // TQ, qkw, TQ), lambda b, s: (b, s, 0, 0)),
            pl.BlockSpec((TM_IN, qkw), row),
            pl.BlockSpec((TM_IN, d), row),
            pl.BlockSpec((TM_IN, d), row),
        ],
        out_shape=[
            jax.ShapeDtypeStruct((n, qkw), BF16),
            jax.ShapeDtypeStruct((batch, seq // TQ, qkw, TQ), BF16),
            jax.ShapeDtypeStruct((n, qkw), BF16),
            jax.ShapeDtypeStruct((n, d), BF16),
            jax.ShapeDtypeStruct((n, d), BF16),
        ],
        scratch_shapes=[pltpu.VMEM((MAX_WINDOW, pw), F32)],
        compiler_params=_cparams(("arbitrary", "arbitrary")),
        name="mixer_in",
    )(xf, g, w_in, b_gate, pool_w, pool_scale, pool_up, qg, kg, cos_t, sa_t, sb_t, seg)


def _attn_route_kernel(lam_init, lam_ref, q_ref, kt_ref, v_ref, x_ref, gp_ref, ga_ref, sg_ref,
                       au_ref, wo_ref, fg_ref, rwh_ref, rwl_ref, rb_ref,
                       x1_ref, h2_ref, ti_ref, tp_ref, rk_ref, cnt_ref,
                       acc_ref, o_ref, carry_ref):
    b = pl.program_id(0)
    qi = pl.program_id(1)
    tq = q_ref.shape[0]
    tk = kt_ref.shape[-1]
    n_exp = rb_ref.shape[1]

    lam_p = lam_ref[...]
    lam = (jnp.exp(jnp.sum(lam_p[0:1] * lam_p[1:2], axis=1, keepdims=True))
           - jnp.exp(jnp.sum(lam_p[2:3] * lam_p[3:4], axis=1, keepdims=True))
           + lam_init)

    lane = lax.broadcasted_iota(I32, (tq, LANES), 1)
    r_chunk = lax.broadcasted_iota(I32, (tq, tk), 0) // CHUNK
    c_chunk = lax.broadcasted_iota(I32, (tq, tk), 1) // CHUNK
    diag_mask = r_chunk >= c_chunk

    for hd in range(N_HEADS):
        cols = slice(hd * LANES, (hd + 1) * LANES)
        qh = q_ref[:, cols]
        zero = jnp.zeros_like(qh)
        qms = (jnp.where(lane < HEAD_DIM, qh, zero), jnp.where(lane >= HEAD_DIM, qh, zero))

        ktd = kt_ref[qi, cols, :]
        vd = v_ref[pl.ds(pl.multiple_of(qi * tk, tk), tk), cols]
        state = []
        for m in range(2):
            s = jnp.where(diag_mask, _dot(qms[m], ktd), MASK_VALUE)
            mx = jnp.max(s, axis=1, keepdims=True)
            p = jnp.exp(s - mx)
            state += [mx, jnp.sum(p, axis=1, keepdims=True)]
            acc_ref[m] = _dot(p.astype(BF16), vd)

        def body(j, carry, qms=qms, cols=cols):
            ktj = kt_ref[j, cols, :]
            vj = v_ref[pl.ds(pl.multiple_of(j * tk, tk), tk), cols]
            new = []
            for m in range(2):
                m_old, l_old = carry[2 * m], carry[2 * m + 1]
                s = _dot(qms[m], ktj)
                m_new = jnp.maximum(m_old, jnp.max(s, axis=1, keepdims=True))
                alpha = jnp.exp(m_old - m_new)
                p = jnp.exp(s - m_new)
                new += [m_new, alpha * l_old + jnp.sum(p, axis=1, keepdims=True)]
                acc_ref[m] = alpha * acc_ref[m] + _dot(p.astype(BF16), vj)
            return tuple(new)

        _, l1, _, l2 = lax.fori_loop(0, qi, body, tuple(state))
        o = acc_ref[0] / l1 - lam * (acc_ref[1] / l2)
        o_ms = jnp.mean(o * o, axis=1, keepdims=True)
        o_ref[:, cols] = o * lax.rsqrt(o_ms + NORM_EPS) * sg_ref[...] * (1.0 - lam_init)

    y_attn = _dot(o_ref[...].astype(BF16), au_ref[...])
    merged = gp_ref[...].astype(F32) + ga_ref[...].astype(F32) * y_attn
    x1 = x_ref[...] + _dot(merged.astype(BF16), wo_ref[...])
    x1_ref[...] = x1

    x_ms = jnp.mean(x1 * x1, axis=1, keepdims=True)
    h2 = x1 * lax.rsqrt(x_ms + NORM_EPS) * fg_ref[...]
    _store_row_tiles(h2_ref, h2)
    hh = h2.astype(BF16)
    hl = (h2 - hh.astype(F32)).astype(BF16)
    logits = (_dot(hh, rwh_ref[...]) + _dot(hh, rwl_ref[...]) + _dot(hl, rwh_ref[...])
              + rb_ref[...])

    e_iota = lax.broadcasted_iota(I32, (tq, n_exp), 1)
    work = logits
    vals, idxs, hots = [], [], []
    for _ in range(TOP_K):
        mx = jnp.max(work, axis=1, keepdims=True)
        idx = jnp.min(jnp.where(work == mx, e_iota, n_exp), axis=1, keepdims=True)
        hot = e_iota == idx
        work = jnp.where(hot, -jnp.inf, work)
        vals.append(mx)
        idxs.append(idx)
        hots.append(hot)
    exps = [jnp.exp(v - vals[0]) for v in vals]
    den = exps[0] + exps[1] + exps[2] + exps[3]

    @pl.when(jnp.logical_and(b == 0, qi == 0))
    def _():
        carry_ref[...] = jnp.zeros_like(carry_ref)

    sel = (hots[0].astype(F32) + hots[1].astype(F32) + hots[2].astype(F32) + hots[3].astype(F32))
    tri = (lax.broadcasted_iota(I32, (tq, tq), 0) > lax.broadcasted_iota(I32, (tq, tq), 1))
    before = _dot(tri.astype(BF16), sel.astype(BF16)) + carry_ref[...]
    carry_ref[...] = carry_ref[...] + jnp.sum(sel, axis=0, keepdims=True)
    cnt_ref[...] = carry_ref[...]

    k_iota = lax.broadcasted_iota(I32, (tq, TOP_K), 1)
    ti = jnp.zeros((tq, TOP_K), I32)
    tp = jnp.zeros((tq, TOP_K), F32)
    rk = jnp.zeros((tq, TOP_K), I32)
    for k in range(TOP_K):
        rank_k = jnp.sum(jnp.where(hots[k], before, 0.0), axis=1, keepdims=True).astype(I32)
        ti = jnp.where(k_iota == k, idxs[k], ti)
        tp = jnp.where(k_iota == k, exps[k] / den, tp)
        rk = jnp.where(k_iota == k, rank_k, rk)
    ti_ref[...] = ti
    tp_ref[...] = tp
    rk_ref[...] = rk


def _attn_route(lam_init, lam_p, q, kt, v, xf, gp, ga, sg, attn_up, w_out, fg, rwh, rwl, rb,
                batch, seq):
    n, d = xf.shape
    qkw = q.shape[1]
    nqt = seq // TQ
    n_exp = rb.shape[1]
    row = lambda b, i: (b * nqt + i, 0)
    const2 = lambda b, i: (0, 0)
    return pl.pallas_call(
        functools.partial(_attn_route_kernel, lam_init),
        grid=(batch, nqt),
        in_specs=[
            pl.BlockSpec(lam_p.shape, const2),
            pl.BlockSpec((TQ, qkw), row),
            pl.BlockSpec((None, nqt, qkw, TQ), lambda b, i: (b, 0, 0, 0)),
            pl.BlockSpec((seq, qkw), lambda b, i: (b, 0)),
            pl.BlockSpec((TQ, d), row),
            pl.BlockSpec((TQ, d), row),
            pl.BlockSpec((TQ, d), row),
            pl.BlockSpec((1, LANES), const2),
            pl.BlockSpec(attn_up.shape, const2),
            pl.BlockSpec(w_out.shape, const2),
            pl.BlockSpec((1, d), const2),
            pl.BlockSpec(rwh.shape, const2),
            pl.BlockSpec(rwl.shape, const2),
            pl.BlockSpec(rb.shape, const2),
        ],
        out_specs=[
            pl.BlockSpec((TQ, d), row),
            pl.BlockSpec((TQ * SUBLANES, LANES), row),
            pl.BlockSpec((TQ, TOP_K), row),
            pl.BlockSpec((TQ, TOP_K), row),
            pl.BlockSpec((TQ, TOP_K), row),
            pl.BlockSpec((1, n_exp), const2),
        ],
        out_shape=[
            jax.ShapeDtypeStruct((n, d), F32),
            jax.ShapeDtypeStruct((n * SUBLANES, LANES), F32),
            jax.ShapeDtypeStruct((n, TOP_K), I32),
            jax.ShapeDtypeStruct((n, TOP_K), F32),
            jax.ShapeDtypeStruct((n, TOP_K), I32),
            jax.ShapeDtypeStruct((1, n_exp), F32),
        ],
        scratch_shapes=[
            pltpu.VMEM((2, TQ, LANES), F32),
            pltpu.VMEM((TQ, qkw), F32),
            pltpu.VMEM((1, n_exp), F32),
        ],
        compiler_params=_cparams(("arbitrary", "arbitrary")),
        name="attn_route",
    )(lam_p, q, kt, v, xf, gp, ga, sg, attn_up, w_out, fg, rwh, rwl, rb)


def _tile_rows(ref, row, n_rows=1):
    return ref.at[pl.ds(pl.multiple_of(row * SUBLANES, SUBLANES), n_rows * SUBLANES)]


def _push_rows_kernel(dest_ref, last_ref, nv_ref, h_ref, xs_ref, zero_ref, sem, zsem):
    tq = h_ref.shape[0] // SUBLANES
    step = pl.program_id(0)
    base = step * (tq * TOP_K)
    n_tiles = xs_ref.shape[0] // (TM_EXP * SUBLANES)

    @pl.when(step == 0)
    def _():
        zero_ref[...] = jnp.zeros_like(zero_ref)

        def zero_tile(t):
            return pltpu.make_async_copy(zero_ref, _tile_rows(xs_ref, t * TM_EXP, TM_EXP), zsem)

        for e in range(N_EXPERTS):
            @pl.when(last_ref[e] >= 0)
            def _(e=e):
                zero_tile(last_ref[e]).start()

        def start_tail(t, c):
            zero_tile(t).start()
            return c

        lax.fori_loop(nv_ref[0], n_tiles, start_tail, 0)
        for e in range(N_EXPERTS):
            @pl.when(last_ref[e] >= 0)
            def _(e=e):
                zero_tile(0).wait()

        def wait_tail(t, c):
            zero_tile(0).wait()
            return c

        lax.fori_loop(nv_ref[0], n_tiles, wait_tail, 0)

    def issue(r, c):
        for k in range(TOP_K):
            d = dest_ref[base + r * TOP_K + k]
            pltpu.make_async_copy(_tile_rows(h_ref, r), _tile_rows(xs_ref, d), sem).start()
        return c

    lax.fori_loop(0, tq, issue, 0)

    def drain(r, c):
        for k in range(TOP_K):
            pltpu.make_async_copy(_tile_rows(h_ref, 0), _tile_rows(xs_ref, 0), sem).wait()
        return c

    lax.fori_loop(0, tq, drain, 0)


def _push_rows(dest, last_tile, n_valid, h2, n_slots):
    lines = h2.shape[0]
    return pl.pallas_call(
        _push_rows_kernel,
        grid_spec=pltpu.PrefetchScalarGridSpec(
            num_scalar_prefetch=3,
            grid=(lines // (TQ * SUBLANES),),
            in_specs=[pl.BlockSpec((TQ * SUBLANES, LANES), lambda i, dest, last, nv: (i, 0))],
            out_specs=pl.BlockSpec(memory_space=pl.ANY),
            scratch_shapes=[
                pltpu.VMEM((TM_EXP * SUBLANES, LANES), F32),
                pltpu.SemaphoreType.DMA(()),
                pltpu.SemaphoreType.DMA(()),
            ],
        ),
        out_shape=jax.ShapeDtypeStruct((n_slots * SUBLANES, LANES), F32),
        compiler_params=_cparams(("arbitrary",)),
        name="push_rows",
    )(dest, last_tile, n_valid, h2)


def _expert_ffn_kernel(te_ref, nv_ref, x_ref, wg_ref, bg_ref, wu_ref, bu_ref, wd_ref, bd_ref,
                       y_ref, wgb_ref, wub_ref, wdb_ref):
    i = pl.program_id(0)

    @pl.when(i >= nv_ref[0])
    def _():
        y_ref[...] = jnp.zeros_like(y_ref)

    @pl.when(i < nv_ref[0])
    def _():
        prev = te_ref[jnp.maximum(i - 1, 0)]

        @pl.when(jnp.logical_or(i == 0, te_ref[i] != prev))
        def _():
            wgb_ref[...] = wg_ref[...].astype(BF16)
            wub_ref[...] = wu_ref[...].astype(BF16)
            wdb_ref[...] = wd_ref[...].astype(BF16)

        x = jnp.concatenate([_load_row_tile_piece(x_ref, j).astype(BF16) for j in range(SUBLANES)],
                            axis=1)
        g = jnp.minimum(_dot(x, wgb_ref[...]) + bg_ref[...], SWIGLU_LIMIT)
        u = jnp.clip(_dot(x, wub_ref[...]) + bu_ref[...], -SWIGLU_LIMIT, SWIGLU_LIMIT)
        act = (u + 1.0) * (g * jax.nn.sigmoid(SWIGLU_ALPHA * g))
        _store_row_tiles(y_ref, _dot(act.astype(BF16), wdb_ref[...]) + bd_ref[...])


def _expert_ffn(tile_expert, n_valid, xs, w_gate, b_gate, w_up, b_up, w_down, b_down):
    n_exp, d, f = w_gate.shape
    assert d == SUBLANES * LANES
    wspec = lambda shape: pl.BlockSpec((None,) + shape, lambda i, te, nv: (te[i], 0, 0))
    x_map = lambda i, te, nv: (jnp.minimum(i, nv[0] - 1), 0)
    return pl.pallas_call(
        _expert_ffn_kernel,
        grid_spec=pltpu.PrefetchScalarGridSpec(
            num_scalar_prefetch=2,
            grid=(xs.shape[0] // (TM_EXP * SUBLANES),),
            in_specs=[
                pl.BlockSpec((TM_EXP * SUBLANES, LANES), x_map),
                wspec((d, f)), wspec((1, f)),
                wspec((d, f)), wspec((1, f)),
                wspec((f, d)), wspec((1, d)),
            ],
            out_specs=pl.BlockSpec((TM_EXP * SUBLANES, LANES), lambda i, te, nv: (i, 0)),
            scratch_shapes=[
                pltpu.VMEM((d, f), BF16),
                pltpu.VMEM((d, f), BF16),
                pltpu.VMEM((f, d), BF16),
            ],
        ),
        out_shape=jax.ShapeDtypeStruct(xs.shape, F32),
        compiler_params=_cparams(("arbitrary",)),
        name="expert_ffn",
    )(tile_expert, n_valid, xs, w_gate, b_gate.reshape(n_exp, 1, f), w_up,
      b_up.reshape(n_exp, 1, f), w_down, b_down.reshape(n_exp, 1, d))


def _combine_kernel(dest_ref, x1_ref, tp_ref, y_ref, out_ref, buf_ref, sem):
    tq = x1_ref.shape[0]
    base = pl.program_id(0) * (tq * TOP_K)

    def issue(r, c):
        for k in range(TOP_K):
            d = dest_ref[base + r * TOP_K + k]
            pltpu.make_async_copy(_tile_rows(y_ref, d), _tile_rows(buf_ref.at[k], r), sem).start()
        return c

    lax.fori_loop(0, tq, issue, 0)

    def drain(r, c):
        for k in range(TOP_K):
            pltpu.make_async_copy(_tile_rows(y_ref, 0), _tile_rows(buf_ref.at[0], 0), sem).wait()
        return c

    lax.fori_loop(0, tq, drain, 0)

    tp = tp_ref[...]
    for j in range(SUBLANES):
        cols = slice(j * LANES, (j + 1) * LANES)
        out = x1_ref[:, cols]
        for k in range(TOP_K):
            out = out + tp[:, k:k + 1] * _load_row_tile_piece(buf_ref.at[k], j)
        out_ref[:, cols] = out


def _combine(dest, x1, tp, y):
    n, d = x1.shape
    return pl.pallas_call(
        _combine_kernel,
        grid_spec=pltpu.PrefetchScalarGridSpec(
            num_scalar_prefetch=1,
            grid=(n // TQ,),
            in_specs=[
                pl.BlockSpec((TQ, d), lambda i, dest: (i, 0)),
                pl.BlockSpec((TQ, TOP_K), lambda i, dest: (i, 0)),
                pl.BlockSpec(memory_space=pl.ANY),
            ],
            out_specs=pl.BlockSpec((TQ, d), lambda i, dest: (i, 0)),
            scratch_shapes=[pltpu.VMEM((TOP_K, TQ * SUBLANES, LANES), F32),
                            pltpu.SemaphoreType.DMA(())],
        ),
        out_shape=jax.ShapeDtypeStruct((n, d), F32),
        compiler_params=_cparams(("arbitrary",)),
        name="combine",
    )(dest, x1, tp, y)


def _rotary_lane_tables(seq):
    half = ROT_DIM // 2
    inv_freq = ROPE_THETA ** (-jnp.arange(0, ROT_DIM, 2, dtype=F32) / ROT_DIM)
    ang = jnp.arange(seq, dtype=F32)[:, None] * inv_freq[None, :]
    cos, sin = jnp.cos(ang), jnp.sin(ang)
    seg_pos = jnp.arange(LANES) % HEAD_DIM
    fidx = seg_pos % half
    cos_l, sin_l = cos[:, fidx], sin[:, fidx]
    lo = (seg_pos < half)[None, :]
    hi = jnp.logical_and(seg_pos >= half, seg_pos < ROT_DIM)[None, :]
    cos_t = jnp.where(jnp.logical_or(lo, hi), cos_l, 1.0)
    sa_t = jnp.where(lo, -sin_l, 0.0)
    sb_t = jnp.where(hi, sin_l, 0.0)
    return cos_t.astype(F32), sa_t.astype(F32), sb_t.astype(F32)


def _routing_tables(ti, rk, cnt, n_tiles):
    counts = cnt[0].astype(I32)
    tiles_e = (counts + TM_EXP - 1) // TM_EXP
    tile_end = jnp.cumsum(tiles_e)
    group_start = (tile_end - tiles_e) * TM_EXP
    e_ids = jnp.arange(N_EXPERTS, dtype=I32)
    start_of = jnp.sum(jnp.where(ti[..., None] == e_ids, group_start, 0), axis=-1)
    dest = (start_of + rk).reshape(-1).astype(I32)
    n_valid = tile_end[-1:].astype(I32)
    tids = jnp.arange(n_tiles, dtype=I32)
    tile_expert = jnp.sum((jnp.minimum(tids, n_valid[0] - 1)[:, None] >= tile_end[None, :]).astype(I32),
                          axis=1).astype(I32)
    last_tile = jnp.where(tiles_e > 0, tile_end - 1, -1).astype(I32)
    return dest, tile_expert, n_valid, last_tile


def kernel(x, mix_norm_g, w_in, b_branch_gate, pool_w, pool_scale, pool_up, q_norm_g, k_norm_g,
           lambda_q1, lambda_k1, lambda_q2, lambda_k2, subln_g, attn_up, w_out, ffn_norm_g,
           router_w, router_b, exp_w_gate, exp_b_gate, exp_w_up, exp_b_up, exp_w_down, exp_b_down):
    batch, seq, d = x.shape
    n = batch * seq
    depth = w_in.shape[0]
    assert seq % TM_IN == 0 and TM_IN % TQ == 0 and HEAD_DIM * 2 == LANES and d == SUBLANES * LANES
    assert (n * TOP_K) % TM_EXP == 0
    n_tiles = (n * TOP_K) // TM_EXP + N_EXPERTS

    cos_t, sa_t, sb_t = _rotary_lane_tables(seq)
    seg = jnp.where((jnp.arange(LANES)[:, None] // HEAD_DIM) == (jnp.arange(LANES)[None, :] // HEAD_DIM),
                    1.0 / HEAD_DIM, 0.0).astype(BF16)
    tile_lanes = lambda g: jnp.tile(g, LANES // HEAD_DIM)[None, :]

    xf = x.reshape(n, d)
    for l in range(depth):
        lam_init = 0.8 - 0.6 * math.exp(-0.3 * l)
        q, kt, v, gp, ga = _mixer_in(
            xf, mix_norm_g[l][None, :], w_in[l].astype(BF16), b_branch_gate[l][None, :],
            pool_w[l].astype(BF16), pool_scale[l][None, :], pool_up[l].astype(BF16),
            tile_lanes(q_norm_g[l]), tile_lanes(k_norm_g[l]), cos_t, sa_t, sb_t, seg, batch, seq)

        lam_p = jnp.stack([lambda_q1[l], lambda_k1[l], lambda_q2[l], lambda_k2[l]]).astype(F32)
        rw = router_w[l]
        rwh = rw.astype(BF16)
        rwl = (rw - rwh.astype(F32)).astype(BF16)
        x1, h2, ti, tp, rk, cnt = _attn_route(
            lam_init, lam_p, q, kt, v, xf, gp, ga, subln_g[l][None, :], attn_up[l].astype(BF16),
            w_out[l].astype(BF16), ffn_norm_g[l][None, :], rwh, rwl, router_b[l][None, :],
            batch, seq)

        dest, tile_expert, n_valid, last_tile = _routing_tables(ti, rk, cnt, n_tiles)
        xs = _push_rows(dest, last_tile, n_valid, h2, n_tiles * TM_EXP)
        y = _expert_ffn(tile_expert, n_valid, xs, exp_w_gate[l], exp_b_gate[l], exp_w_up[l],
                        exp_b_up[l], exp_w_down[l], exp_b_down[l])
        xf = _combine(dest, x1, tp, y)
    return xf.reshape(batch, seq, d)
```

```python
import functools
import math

import jax
import jax.numpy as jnp
from jax import lax
from jax.experimental import pallas as pl
from jax.experimental.pallas import tpu as pltpu

F32 = jnp.float32
BF16 = jnp.bfloat16
I32 = jnp.int32
U32 = jnp.uint32

CHUNK = 64
POOL_WINDOWS = (2, 4, 8, 16)
MAX_WINDOW = max(POOL_WINDOWS)
N_HEADS = 4
HEAD_DIM = 64
ROT_DIM = HEAD_DIM // 4
ROPE_THETA = 500000.0
TOP_K = 4
SWIGLU_LIMIT = 7.0
SWIGLU_ALPHA = 1.702
NORM_EPS = 1e-6
MASK_VALUE = -1e30
COUNT_DIGIT = 32.0

LANES = 128
SUBLANES = 8
PACK_LINES = 4
VMEM_LIMIT_BYTES = 56 * 1024 * 1024
TM_IN = 1024
TQ = 256
TM_EXP = 256
REGION_ROWS = TQ * TOP_K + LANES
COMBINE_STEPS = 4
DOWN_CHUNK = 256
FAST_PIECES = 12
X_AHEAD = 3
Y_RING = 3


def _cparams(sem):
    return pltpu.CompilerParams(dimension_semantics=sem, vmem_limit_bytes=VMEM_LIMIT_BYTES)


def _dot(a, b):
    return jnp.dot(a, b, preferred_element_type=F32)


def _store_packed_rows(ref, val, first_piece=0):
    rows, width = val.shape
    assert ref.shape == (rows * PACK_LINES, LANES) and first_piece * LANES + width <= SUBLANES * LANES
    for j in range(width // LANES):
        words = pltpu.bitcast(val[:, j * LANES:(j + 1) * LANES].astype(BF16), U32)
        ref[pl.ds(first_piece + j, rows // 2, stride=SUBLANES), :] = words


def _load_packed_pieces(ref):
    pairs = ref.shape[0] // SUBLANES
    return [pltpu.bitcast(ref[pl.ds(j, pairs, stride=SUBLANES), :], BF16)
            for j in range(SUBLANES)]


def _mixer_in_kernel(x_ref, g_ref, w_ref, bg_ref, pw_ref, ps_ref, pu_ref, qg_ref, kg_ref,
                     cos_ref, sa_ref, sb_ref, seg_ref,
                     qt_ref, k_ref, vt_ref, gp_ref, ga_ref, carry_ref):
    si = pl.program_id(1)
    tm, d = x_ref.shape
    pw_width = ps_ref.shape[1]
    qk_width = k_ref.shape[1]
    tk = qt_ref.shape[-1]

    @pl.when(si == 0)
    def _():
        carry_ref[...] = jnp.zeros_like(carry_ref)

    x = x_ref[...]
    ms = jnp.mean(x * x, axis=-1, keepdims=True)
    h = (x * lax.rsqrt(ms + NORM_EPS) * g_ref[...]).astype(BF16)

    o_q = pw_width
    o_k = o_q + qk_width
    o_v = o_k + qk_width
    o_g = o_v + vt_ref.shape[1]

    def proj(lo, hi):
        return _dot(h, w_ref[:, lo:hi])

    u = proj(0, o_q)
    ga_ref[...] = jax.nn.sigmoid(proj(o_g + d, o_g + 2 * d) + bg_ref[:, d:]).astype(BF16)
    g_pool = jax.nn.sigmoid(proj(o_g, o_g + d) + bg_ref[:, :d])

    ext = jnp.concatenate([carry_ref[...], u], axis=0)
    carry_ref[...] = u[tm - MAX_WINDOW:, :]
    pos = si * tm + lax.broadcasted_iota(I32, (tm, 1), 0)
    ys = []
    for gi, win in enumerate(POOL_WINDOWS):
        s = ext[:, gi * LANES:(gi + 1) * LANES]
        shift = 1
        while shift < win:
            s = s + pltpu.roll(s, shift, axis=0)
            shift *= 2
        cnt = jnp.minimum(pos + 1, win).astype(F32)
        p = s[MAX_WINDOW:, :] / cnt - u[:, gi * LANES:(gi + 1) * LANES]
        ys.append(_dot(p.astype(BF16), pw_ref[gi]))
    yp = jnp.concatenate(ys, axis=1) * ps_ref[...]
    y_pool = _dot(yp.astype(BF16), pu_ref[...])
    gp_ref[...] = (g_pool * y_pool).astype(BF16)

    def qk_prep(t, gain_ref, scale):
        outs = []
        for c in range(qk_width // LANES):
            tc = t[:, c * LANES:(c + 1) * LANES]
            seg_ms = _dot((tc * tc).astype(BF16), seg_ref[...])
            n = tc * lax.rsqrt(seg_ms + NORM_EPS) * gain_ref[...]
            r = (n * cos_ref[...]
                 + pltpu.roll(n, LANES - ROT_DIM // 2, axis=1) * sa_ref[...]
                 + pltpu.roll(n, ROT_DIM // 2, axis=1) * sb_ref[...])
            outs.append(r * scale)
        return jnp.concatenate(outs, axis=1)

    qt = qk_prep(proj(o_q, o_k), qg_ref, HEAD_DIM ** -0.5 * math.log2(math.e)).T
    for j in range(tm // tk):
        qt_ref[j] = qt[:, j * tk:(j + 1) * tk].astype(BF16)
    k_ref[...] = qk_prep(proj(o_k, o_v), kg_ref, 1.0).astype(BF16)
    vt = proj(o_v, o_g).T
    for j in range(tm // tk):
        vt_ref[j] = vt[:, j * tk:(j + 1) * tk].astype(BF16)


def _mixer_in(xf, g, w_in, b_gate, pool_w, pool_scale, pool_up, qg, kg, cos_t, sa_t, sb_t, seg,
              batch, seq):
    n, d = xf.shape
    pw = pool_scale.shape[1]
    qkw = (w_in.shape[1] - pw - 2 * d) // 3
    nst = seq // TM_IN
    row = lambda b, s: (b * nst + s, 0)
    const2 = lambda b, s: (0, 0)
    return pl.pallas_call(
        _mixer_in_kernel,
        grid=(batch, nst),
        in_specs=[
            pl.BlockSpec((TM_IN, d), row),
            pl.BlockSpec((1, d), const2),
            pl.BlockSpec(w_in.shape, const2),
            pl.BlockSpec(b_gate.shape, const2),
            pl.BlockSpec(pool_w.shape, lambda b, s: (0, 0, 0)),
            pl.BlockSpec(pool_scale.shape, const2),
            pl.BlockSpec(pool_up.shape, const2),
            pl.BlockSpec((1, LANES), const2),
            pl.BlockSpec((1, LANES), const2),
            pl.BlockSpec((TM_IN, LANES), lambda b, s: (s, 0)),
            pl.BlockSpec((TM_IN, LANES), lambda b, s: (s, 0)),
            pl.BlockSpec((TM_IN, LANES), lambda b, s: (s, 0)),
            pl.BlockSpec((LANES, LANES), const2),
        ],
        out_specs=[
            pl.BlockSpec((None, TM_IN // TQ, qkw, TQ), lambda b, s: (b, s, 0, 0)),
            pl.BlockSpec((TM_IN, qkw), row),
            pl.BlockSpec((None, TM_IN // TQ, qkw, TQ), lambda b, s: (b, s, 0, 0)),
            pl.BlockSpec((TM_IN, d), row),
            pl.BlockSpec((TM_IN, d), row),
        ],
        out_shape=[
            jax.ShapeDtypeStruct((batch, seq // TQ, qkw, TQ), BF16),
            jax.ShapeDtypeStruct((n, qkw), BF16),
            jax.ShapeDtypeStruct((batch, seq // TQ, qkw, TQ), BF16),
            jax.ShapeDtypeStruct((n, d), BF16),
            jax.ShapeDtypeStruct((n, d), BF16),
        ],
        scratch_shapes=[pltpu.VMEM((MAX_WINDOW, pw), F32)],
        compiler_params=_cparams(("arbitrary", "arbitrary")),
        name="mixer_in",
    )(xf, g, w_in, b_gate, pool_w, pool_scale, pool_up, qg, kg, cos_t, sa_t, sb_t, seg)


def _attn_route_kernel(lam_init, lam_ref, qt_ref, k_ref, vt_ref, x_ref, gp_ref, ga_ref, sg_ref,
                       au_ref, wo_ref, fg_ref, rw_ref, rb_ref,
                       x1_ref, xl_ref, tp_ref, pos_ref, cnt_ref,
                       acc_ref, qm_ref, s_ref, m_ref, l_ref, o_ref):
    qi = pl.program_id(1)
    tq = qt_ref.shape[1]
    tk = vt_ref.shape[-1]
    n_exp = cnt_ref.shape[0]

    lam_p = lam_ref[...]
    lam = (jnp.exp(jnp.sum(lam_p[0:1] * lam_p[1:2], axis=1, keepdims=True))
           - jnp.exp(jnp.sum(lam_p[2:3] * lam_p[3:4], axis=1, keepdims=True))
           + lam_init)

    feat = lax.broadcasted_iota(I32, (LANES, tq), 0)
    k_chunk = lax.broadcasted_iota(I32, (tk, tq), 0) // CHUNK
    q_chunk = lax.broadcasted_iota(I32, (tk, tq), 1) // CHUNK
    diag_mask = q_chunk >= k_chunk

    chains = [(hd, m) for hd in range(N_HEADS) for m in range(2)]
    head_cols = lambda hd: slice(hd * LANES, (hd + 1) * LANES)

    def attend(j, first):
        k_rows = pl.ds(pl.multiple_of(j * tk, tk), tk)
        for c, (hd, m) in enumerate(chains):
            s_ref[c] = _dot(k_ref[k_rows, head_cols(hd)], qm_ref[c])
        for c, (hd, m) in enumerate(chains):
            s = s_ref[c]
            if first:
                s = jnp.where(diag_mask, s, MASK_VALUE)
                m_new = jnp.max(s, axis=0, keepdims=True)
            else:
                m_old = m_ref[c:c + 1, :]
                m_new = jnp.maximum(m_old, jnp.max(s, axis=0, keepdims=True))
            p = jnp.exp2(s - m_new)
            pv = _dot(vt_ref[j, head_cols(hd), :], p.astype(BF16))
            if first:
                l_ref[c:c + 1, :] = jnp.sum(p, axis=0, keepdims=True)
                acc_ref[c] = pv
            else:
                alpha = jnp.exp2(m_old - m_new)
                l_ref[c:c + 1, :] = alpha * l_ref[c:c + 1, :] + jnp.sum(p, axis=0, keepdims=True)
                acc_ref[c] = alpha * acc_ref[c] + pv
            m_ref[c:c + 1, :] = m_new

    for c, (hd, m) in enumerate(chains):
        qth = qt_ref[head_cols(hd), :]
        keep = (feat >= HEAD_DIM) if m else (feat < HEAD_DIM)
        qm_ref[c] = jnp.where(keep, qth, jnp.zeros_like(qth))

    attend(qi, True)

    def body(j, carry):
        attend(j, False)
        return carry

    lax.fori_loop(0, qi, body, 0)

    for hd in range(N_HEADS):
        c1, c2 = 2 * hd, 2 * hd + 1
        ot = (acc_ref[c1] * (1.0 / l_ref[c1:c1 + 1, :])
              - (lam / l_ref[c2:c2 + 1, :]) * acc_ref[c2])
        o_ms = jnp.mean(ot * ot, axis=0, keepdims=True)
        o = (ot * lax.rsqrt(o_ms + NORM_EPS)).T
        o_ref[:, head_cols(hd)] = o * sg_ref[...] * (1.0 - lam_init)

    y_attn = _dot(o_ref[...].astype(BF16), au_ref[...])
    merged = gp_ref[...].astype(F32) + ga_ref[...].astype(F32) * y_attn
    x1 = x_ref[...] + _dot(merged.astype(BF16), wo_ref[...])
    x1_ref[...] = x1

    x_ms = jnp.mean(x1 * x1, axis=1, keepdims=True)
    h2 = x1 * lax.rsqrt(x_ms + NORM_EPS) * fg_ref[...]
    hh = h2.astype(BF16)
    hl = (h2 - hh.astype(F32)).astype(BF16)
    hw = _dot(hh, rw_ref[...])
    logits = hw[:, :LANES] + hw[:, LANES:] + _dot(hl, rw_ref[:, :LANES]) + rb_ref[...]

    work = logits.T[:n_exp, :]
    e_iota = lax.broadcasted_iota(I32, (n_exp, tq), 0)
    vals, hots = [], []
    for _ in range(TOP_K):
        mx = jnp.max(work, axis=0, keepdims=True)
        idx = jnp.min(jnp.where(work == mx, e_iota, n_exp), axis=0, keepdims=True)
        hot = e_iota == idx
        work = jnp.where(hot, -jnp.inf, work)
        vals.append(mx)
        hots.append(hot)
    exps = [jnp.exp(v - vals[0]) for v in vals]
    den = exps[0] + exps[1] + exps[2] + exps[3]

    sel = (hots[0].astype(F32) + hots[1].astype(F32) + hots[2].astype(F32) + hots[3].astype(F32))
    sel_b = sel.astype(BF16)
    earlier = (lax.broadcasted_iota(I32, (tq, tq), 0) < lax.broadcasted_iota(I32, (tq, tq), 1))
    lower = (lax.broadcasted_iota(I32, (n_exp, n_exp), 0)
             > lax.broadcasted_iota(I32, (n_exp, n_exp), 1))
    cnt = jnp.sum(sel, axis=1, keepdims=True)
    even = jnp.broadcast_to(cnt + (cnt - 2.0 * jnp.floor(cnt * 0.5)), (n_exp, LANES))
    even_hi = jnp.floor(even * (1.0 / COUNT_DIGIT))
    lower_b = jnp.where(lower, 1.0, 0.0).astype(BF16)
    seg_start = (COUNT_DIGIT * _dot(lower_b, even_hi.astype(BF16))
                 + _dot(lower_b, (even - COUNT_DIGIT * even_hi).astype(BF16)))[:, :1]
    slot_of = _dot(sel_b, jnp.where(earlier, 1.0, 0.0).astype(BF16)) + seg_start
    cnt_ref[...] = even[:, :1]

    k_iota = lax.broadcasted_iota(I32, (TOP_K, tq), 0)
    tp = jnp.zeros((TOP_K, tq), F32)
    pos = jnp.zeros((TOP_K, tq), I32)
    slots = []
    for k in range(TOP_K):
        slot_k = jnp.sum(jnp.where(hots[k], slot_of, 0.0), axis=0, keepdims=True).astype(I32)
        slots.append(slot_k)
        tp = jnp.where(k_iota == k, exps[k] / den, tp)
        pos = jnp.where(k_iota == k, slot_k, pos)
    tp_ref[...] = tp
    pos_ref[...] = pos

    chunk = LANES
    for c in range(xl_ref.shape[0] // (chunk * PACK_LINES)):
        p_iota = lax.broadcasted_iota(I32, (chunk, tq), 0) + c * chunk
        hit = ((p_iota == slots[0]) | (p_iota == slots[1])
               | (p_iota == slots[2]) | (p_iota == slots[3]))
        rows_c = _dot(jnp.where(hit, 1.0, 0.0).astype(BF16), hh)
        _store_packed_rows(xl_ref.at[pl.ds(c * chunk * PACK_LINES, chunk * PACK_LINES)], rows_c)


def _attn_route(lam_init, lam_p, qt, k, vt, xf, gp, ga, sg, attn_up, w_out, fg, rw, rb,
                n_exp, batch, seq):
    n, d = xf.shape
    qkw = k.shape[1]
    nqt = seq // TQ
    row = lambda b, i: (b * nqt + i, 0)
    col = lambda b, i: (0, b * nqt + i)
    const2 = lambda b, i: (0, 0)
    return pl.pallas_call(
        functools.partial(_attn_route_kernel, lam_init),
        grid=(batch, nqt),
        in_specs=[
            pl.BlockSpec(lam_p.shape, const2),
            pl.BlockSpec((None, None, qkw, TQ), lambda b, i: (b, i, 0, 0)),
            pl.BlockSpec((seq, qkw), lambda b, i: (b, 0)),
            pl.BlockSpec((None, nqt, qkw, TQ), lambda b, i: (b, 0, 0, 0)),
            pl.BlockSpec((TQ, d), row),
            pl.BlockSpec((TQ, d), row),
            pl.BlockSpec((TQ, d), row),
            pl.BlockSpec((1, LANES), const2),
            pl.BlockSpec(attn_up.shape, const2),
            pl.BlockSpec(w_out.shape, const2),
            pl.BlockSpec((1, d), const2),
            pl.BlockSpec(rw.shape, const2),
            pl.BlockSpec(rb.shape, const2),
        ],
        out_specs=[
            pl.BlockSpec((TQ, d), row),
            pl.BlockSpec((REGION_ROWS * PACK_LINES, LANES), row),
            pl.BlockSpec((TOP_K, TQ), col),
            pl.BlockSpec((TOP_K, TQ), col),
            pl.BlockSpec((None, n_exp, 1), lambda b, i: (b * nqt + i, 0, 0)),
        ],
        out_shape=[
            jax.ShapeDtypeStruct((n, d), F32),
            jax.ShapeDtypeStruct((batch * nqt * REGION_ROWS * PACK_LINES, LANES), U32),
            jax.ShapeDtypeStruct((TOP_K, n), F32),
            jax.ShapeDtypeStruct((TOP_K, n), I32),
            jax.ShapeDtypeStruct((batch * nqt, n_exp, 1), F32),
        ],
        scratch_shapes=[
            pltpu.VMEM((2 * N_HEADS, LANES, TQ), F32),
            pltpu.VMEM((2 * N_HEADS, LANES, TQ), BF16),
            pltpu.VMEM((2 * N_HEADS, TQ, TQ), F32),
            pltpu.VMEM((2 * N_HEADS, TQ), F32),
            pltpu.VMEM((2 * N_HEADS, TQ), F32),
            pltpu.VMEM((TQ, qkw), F32),
        ],
        compiler_params=_cparams(("arbitrary", "arbitrary")),
        name="attn_route",
    )(lam_p, qt, k, vt, xf, gp, ga, sg, attn_up, w_out, fg, rw, rb)


def _row_lines(ref, row, n_rows):
    return ref.at[pl.ds(pl.multiple_of(row * PACK_LINES, SUBLANES), n_rows * PACK_LINES)]


def _expert_ffn_kernel(first_ref, tiles_ref, rsrc_ref, rrank_ref, rcnt_ref, tlo_ref, thi_ref,
                       trows_ref, used_ref, texp_ref, nv_ref, psrc_ref, pdst_ref, pn_ref,
                       xl_ref, wg_ref, bg_ref, wu_ref, bu_ref,
                       wd_ref, bd_ref, yl_ref, wgu_ref, wdb_ref, xbuf_ref, ybuf_ref, zero_ref,
                       xsem, ysem, zsem):
    e = pl.program_id(0)
    n_exp = pl.num_programs(0)
    n_runs = rcnt_ref.shape[0] // n_exp
    nt = tiles_ref[e]
    g0 = first_ref[e]
    nv = nv_ref[0]

    def for_each_piece(g, fn, enabled=True):
        te = texp_ref[g]
        tile_lo = (g - first_ref[te]) * TM_EXP

        r_lo, r_hi = tlo_ref[g], thi_ref[g]

        def piece(r, live):
            idx = te * n_runs + r
            rank0 = rrank_ref[idx]
            lo = jnp.maximum(rank0, tile_lo)
            hi = jnp.minimum(rank0 + rcnt_ref[idx], tile_lo + TM_EXP)

            @pl.when(jnp.logical_and(live, hi > lo))
            def _():
                fn(rsrc_ref[idx] + (lo - rank0), lo - tile_lo, hi - lo)

        for p in range(FAST_PIECES):
            n_rows = pn_ref[g * FAST_PIECES + p]

            @pl.when(jnp.logical_and(enabled, n_rows > 0))
            def _(p=p, n_rows=n_rows):
                fn(psrc_ref[g * FAST_PIECES + p], pdst_ref[g * FAST_PIECES + p], n_rows)

        def run(r, c):
            piece(r, True)
            return c

        lax.fori_loop(r_lo + FAST_PIECES, jnp.where(enabled, r_hi, 0), run, 0)

    xslot = lambda g: g % (X_AHEAD + 1)
    yslot = lambda g: g % Y_RING

    def gather(g, enabled=True):
        for_each_piece(g, lambda src, dst, n: pltpu.make_async_copy(
            _row_lines(xl_ref, src, n), _row_lines(xbuf_ref.at[xslot(g)], dst, n),
            xsem.at[xslot(g)]).start(), enabled)

    def gather_wait(g):
        n = trows_ref[g]
        pltpu.make_async_copy(_row_lines(xl_ref, 0, n), _row_lines(xbuf_ref.at[xslot(g)], 0, n),
                              xsem.at[xslot(g)]).wait()

    def scatter(g):
        for_each_piece(g, lambda dst, src, n: pltpu.make_async_copy(
            _row_lines(ybuf_ref.at[yslot(g)], src, n), _row_lines(yl_ref, dst, n),
            ysem.at[yslot(g)]).start())

    def scatter_wait(g):
        n = trows_ref[g]
        pltpu.make_async_copy(_row_lines(ybuf_ref.at[yslot(g)], 0, n), _row_lines(yl_ref, 0, n),
                              ysem.at[yslot(g)]).wait()

    @pl.when(e == 0)
    def _():
        xbuf_ref[...] = jnp.zeros_like(xbuf_ref)
        zero_ref[...] = jnp.zeros_like(zero_ref)
        n_steps = used_ref.shape[0]

        def unused(step):
            used = used_ref[step]
            return pltpu.make_async_copy(
                _row_lines(zero_ref, 0, REGION_ROWS - used),
                _row_lines(yl_ref, step * REGION_ROWS + used, REGION_ROWS - used), zsem)

        def clear(step, c):
            unused(step).start()
            return c

        def clear_wait(step, c):
            unused(step).wait()
            return c

        lax.fori_loop(0, n_steps, clear, 0)
        lax.fori_loop(0, n_steps, clear_wait, 0)
        for ahead in range(X_AHEAD):
            @pl.when(ahead < nv)
            def _(ahead=ahead):
                gather(ahead)

    @pl.when(nt > 0)
    def _():
        f = wg_ref.shape[1]
        wgu_ref[:, :f] = wg_ref[...].astype(BF16)
        wgu_ref[:, f:] = wu_ref[...].astype(BF16)
        wdb_ref[...] = wd_ref[...].astype(BF16)

        def tile(i, c):
            g = g0 + i
            gather_wait(g)

            @pl.when(g >= Y_RING)
            def _():
                scatter_wait(g - Y_RING)

            gather(jnp.minimum(g + X_AHEAD, nv - 1), g + X_AHEAD < nv)

            x = jnp.concatenate(_load_packed_pieces(xbuf_ref.at[xslot(g)]), axis=1)
            gu = _dot(x, wgu_ref[...])
            gate = jnp.minimum(gu[:, :f] + bg_ref[...], SWIGLU_LIMIT)
            up = jnp.clip(gu[:, f:] + bu_ref[...], -SWIGLU_LIMIT, SWIGLU_LIMIT)
            act = ((up + 1.0) * (gate * jax.nn.sigmoid(SWIGLU_ALPHA * gate))).astype(BF16)
            for lo in range(0, wdb_ref.shape[1], DOWN_CHUNK):
                cols = slice(lo, lo + DOWN_CHUNK)
                _store_packed_rows(ybuf_ref.at[yslot(g)],
                                   _dot(act, wdb_ref[:, cols]) + bd_ref[:, cols], lo // LANES)
            scatter(g)
            return c

        lax.fori_loop(0, nt, tile, 0)

    @pl.when(e == n_exp - 1)
    def _():
        for back in range(Y_RING, 0, -1):
            @pl.when(nv >= back)
            def _(back=back):
                scatter_wait(nv - back)


def _expert_ffn(tables, xl, w_gate, b_gate, w_up, b_up, w_down, b_down):
    n_exp, d, f = w_gate.shape
    assert d == 2 * PACK_LINES * LANES
    wspec = lambda shape: pl.BlockSpec((None,) + shape, lambda e, *tables: (e, 0, 0))
    tile_shape = (TM_EXP * PACK_LINES, LANES)
    return pl.pallas_call(
        _expert_ffn_kernel,
        grid_spec=pltpu.PrefetchScalarGridSpec(
            num_scalar_prefetch=len(tables),
            grid=(n_exp,),
            in_specs=[
                pl.BlockSpec(memory_space=pl.ANY),
                wspec((d, f)), wspec((1, f)),
                wspec((d, f)), wspec((1, f)),
                wspec((f, d)), wspec((1, d)),
            ],
            out_specs=pl.BlockSpec(memory_space=pl.ANY),
            scratch_shapes=[
                pltpu.VMEM((d, 2 * f), BF16),
                pltpu.VMEM((f, d), BF16),
                pltpu.VMEM((X_AHEAD + 1,) + tile_shape, U32),
                pltpu.VMEM((Y_RING,) + tile_shape, U32),
                pltpu.VMEM((LANES * PACK_LINES, LANES), U32),
                pltpu.SemaphoreType.DMA((X_AHEAD + 1,)),
                pltpu.SemaphoreType.DMA((Y_RING,)),
                pltpu.SemaphoreType.DMA(()),
            ],
        ),
        out_shape=jax.ShapeDtypeStruct(xl.shape, U32),
        compiler_params=_cparams(("arbitrary",)),
        name="expert_ffn",
    )(*tables, xl, w_gate, b_gate.reshape(n_exp, 1, f), w_up, b_up.reshape(n_exp, 1, f), w_down,
      b_down.reshape(n_exp, 1, d))


def _combine_kernel(x1_ref, pos_ref, tp_ref, yl_ref, out_ref):
    region_lines = REGION_ROWS * PACK_LINES
    slot_iota = lax.broadcasted_iota(I32, (REGION_ROWS, TQ), 0)
    for r in range(x1_ref.shape[0] // TQ):
        rows = slice(r * TQ, (r + 1) * TQ)
        w = jnp.zeros((REGION_ROWS, TQ), F32)
        for k in range(TOP_K):
            w = w + jnp.where(slot_iota == pos_ref[k:k + 1, rows], tp_ref[k:k + 1, rows], 0.0)
        w = w.astype(BF16)
        pieces = _load_packed_pieces(yl_ref.at[pl.ds(r * region_lines, region_lines)])
        for j in range(0, len(pieces), 2):
            cols = slice(j * LANES, (j + 2) * LANES)
            picked = lax.dot_general(w, jnp.concatenate(pieces[j:j + 2], axis=1),
                                     (((0,), (0,)), ((), ())), preferred_element_type=F32)
            out_ref[rows, cols] = x1_ref[rows, cols] + picked


def _combine(x1, pos_t, tp_t, yl):
    n, d = x1.shape
    row = lambda i: (i, 0)
    col = lambda i: (0, i)
    return pl.pallas_call(
        _combine_kernel,
        grid=(n // (COMBINE_STEPS * TQ),),
        in_specs=[
            pl.BlockSpec((COMBINE_STEPS * TQ, d), row),
            pl.BlockSpec((TOP_K, COMBINE_STEPS * TQ), col),
            pl.BlockSpec((TOP_K, COMBINE_STEPS * TQ), col),
            pl.BlockSpec((COMBINE_STEPS * REGION_ROWS * PACK_LINES, LANES), row),
        ],
        out_specs=pl.BlockSpec((COMBINE_STEPS * TQ, d), row),
        out_shape=jax.ShapeDtypeStruct((n, d), F32),
        compiler_params=_cparams(("arbitrary",)),
        name="combine",
    )(x1, pos_t, tp_t, yl)


def _rotary_lane_tables(seq):
    half = ROT_DIM // 2
    inv_freq = ROPE_THETA ** (-jnp.arange(0, ROT_DIM, 2, dtype=F32) / ROT_DIM)
    ang = jnp.arange(seq, dtype=F32)[:, None] * inv_freq[None, :]
    cos, sin = jnp.cos(ang), jnp.sin(ang)
    seg_pos = jnp.arange(LANES) % HEAD_DIM
    fidx = seg_pos % half
    cos_l, sin_l = cos[:, fidx], sin[:, fidx]
    lo = (seg_pos < half)[None, :]
    hi = jnp.logical_and(seg_pos >= half, seg_pos < ROT_DIM)[None, :]
    cos_t = jnp.where(jnp.logical_or(lo, hi), cos_l, 1.0)
    sa_t = jnp.where(lo, -sin_l, 0.0)
    sb_t = jnp.where(hi, sin_l, 0.0)
    return cos_t.astype(F32), sa_t.astype(F32), sb_t.astype(F32)


def _routing_tables(step_counts):
    cnt = step_counts.astype(I32)
    n_steps, n_exp = cnt.shape
    slots_per_step = REGION_ROWS

    def before(v, axis):
        m = v.shape[axis]
        tri = (jnp.arange(m)[:, None] > jnp.arange(m)[None, :]).astype(I32)
        if axis == 0:
            return jnp.sum(tri[:, :, None] * v[None, :, :], axis=1)
        return jnp.sum(tri[None, :, :] * v[:, None, :], axis=2)

    run_src = jnp.arange(n_steps, dtype=I32)[:, None] * slots_per_step + before(cnt, 1)
    run_rank = before(cnt, 0)
    total = jnp.sum(cnt, axis=0)
    tiles_e = (total + TM_EXP - 1) // TM_EXP
    tile_end = before(tiles_e[None, :], 1)[0] + tiles_e
    first_tile = tile_end - tiles_e
    n_tiles = (n_steps * slots_per_step) // TM_EXP + n_exp
    g = jnp.arange(n_tiles, dtype=I32)[:, None]
    owner = jnp.logical_and(g >= first_tile[None, :], g < tile_end[None, :]).astype(I32)
    pick = lambda per_expert: jnp.sum(owner * per_expert, axis=1)
    lo_rank = pick((g - first_tile[None, :]) * TM_EXP)
    r0 = jnp.sum(owner[:, None, :] * run_rank[None, :, :], axis=2)
    r1 = r0 + jnp.sum(owner[:, None, :] * cnt[None, :, :], axis=2)
    tile_lo = jnp.sum((r1 <= lo_rank[:, None]).astype(I32), axis=1)
    tile_hi = jnp.sum((r0 < lo_rank[:, None] + TM_EXP).astype(I32), axis=1)
    tile_rows = jnp.clip(pick(total[None, :]) - lo_rank, 0, TM_EXP)
    flat = lambda t: t.T.reshape(-1).astype(I32)
    return tuple(t.astype(I32) for t in (first_tile, tiles_e)) + (
        flat(run_src), flat(run_rank), flat(cnt),
        tile_lo.astype(I32), tile_hi.astype(I32), tile_rows.astype(I32),
        jnp.sum(cnt, axis=1).astype(I32), pick(jnp.arange(n_exp, dtype=I32)[None, :]).astype(I32),
        jnp.sum(tiles_e, keepdims=True).astype(I32)) + _first_pieces(
            tile_lo, tile_hi, lo_rank, r0, r1,
            jnp.sum(owner[:, None, :] * run_src[None, :, :], axis=2))


def _first_pieces(tile_lo, tile_hi, lo_rank, r0, r1, src0):
    n_steps = r0.shape[1]
    run = tile_lo[:, None] + jnp.arange(FAST_PIECES, dtype=I32)[None, :]
    hot = (jnp.minimum(run, n_steps - 1)[:, :, None]
           == jnp.arange(n_steps, dtype=I32)[None, None, :]).astype(I32)
    of_run = lambda t: jnp.sum(hot * t[:, None, :], axis=2)
    rank0, rank1 = of_run(r0), of_run(r1)
    lo = jnp.maximum(rank0, lo_rank[:, None])
    hi = jnp.minimum(rank1, lo_rank[:, None] + TM_EXP)
    n_rows = jnp.where(run < tile_hi[:, None], jnp.maximum(hi - lo, 0), 0)
    flat = lambda t: t.reshape(-1).astype(I32)
    return flat(of_run(src0) + lo - rank0), flat(lo - lo_rank[:, None]), flat(n_rows)


def kernel(x, mix_norm_g, w_in, b_branch_gate, pool_w, pool_scale, pool_up, q_norm_g, k_norm_g,
           lambda_q1, lambda_k1, lambda_q2, lambda_k2, subln_g, attn_up, w_out, ffn_norm_g,
           router_w, router_b, exp_w_gate, exp_b_gate, exp_w_up, exp_b_up, exp_w_down, exp_b_down):
    batch, seq, d = x.shape
    n = batch * seq
    depth = w_in.shape[0]
    assert seq % TM_IN == 0 and TM_IN % TQ == 0 and HEAD_DIM * 2 == LANES
    assert d == 2 * PACK_LINES * LANES and TM_EXP % 2 == 0 and REGION_ROWS % LANES == 0

    cos_t, sa_t, sb_t = _rotary_lane_tables(seq)
    seg = jnp.where((jnp.arange(LANES)[:, None] // HEAD_DIM) == (jnp.arange(LANES)[None, :] // HEAD_DIM),
                    1.0 / HEAD_DIM, 0.0).astype(BF16)
    tile_lanes = lambda g: jnp.tile(g, LANES // HEAD_DIM)[None, :]

    xf = x.reshape(n, d)
    for l in range(depth):
        lam_init = 0.8 - 0.6 * math.exp(-0.3 * l)
        qt, k, vt, gp, ga = _mixer_in(
            xf, mix_norm_g[l][None, :], w_in[l].astype(BF16), b_branch_gate[l][None, :],
            pool_w[l].astype(BF16), pool_scale[l][None, :], pool_up[l].astype(BF16),
            tile_lanes(q_norm_g[l]), tile_lanes(k_norm_g[l]), cos_t, sa_t, sb_t, seg, batch, seq)

        lam_p = jnp.stack([lambda_q1[l], lambda_k1[l], lambda_q2[l], lambda_k2[l]]).astype(F32)
        n_exp = router_w.shape[-1]
        rw = jnp.pad(router_w[l], ((0, 0), (0, LANES - n_exp)))
        rwh = rw.astype(BF16)
        rwl = (rw - rwh.astype(F32)).astype(BF16)
        rb = jnp.pad(router_b[l], (0, LANES - n_exp))[None, :]
        x1, xl, tp_t, pos_t, step_counts = _attn_route(
            lam_init, lam_p, qt, k, vt, xf, gp, ga, subln_g[l][None, :], attn_up[l].astype(BF16),
            w_out[l].astype(BF16), ffn_norm_g[l][None, :], jnp.concatenate([rwh, rwl], axis=1), rb,
            n_exp, batch, seq)

        tables = _routing_tables(step_counts[:, :, 0])
        yl = _expert_ffn(tables, xl, exp_w_gate[l], exp_b_gate[l], exp_w_up[l], exp_b_up[l],
                         exp_w_down[l], exp_b_down[l])
        xf = _combine(x1, pos_t, tp_t, yl)
    return xf.reshape(batch, seq, d)
```
